```python
import jax, jax.numpy as jnp
from jax import lax
import numpy as np

D_MODEL = 1024
BATCH = 8
SEQ = 2048
DEPTH = 2
DEC_BATCH = 32
DEC_SEQ = 8
PAST_LEN = 8192
PAGE_SIZE = 128

BRANCH_W = D_MODEL // 2
POOL_WINDOWS = (2, 4, 8, 16)
POOL_GROUPS = len(POOL_WINDOWS)
POOL_GC = BRANCH_W // POOL_GROUPS
POOL_HIST = max(POOL_WINDOWS) - 1
N_HEADS = 8
HEAD_DIM = BRANCH_W // N_HEADS
Q_BLOCK = 128
CONV_W = 3
CONV_HIST = CONV_W - 1
N_BRANCH = 3
RMS_EPS = 1e-6
FORGET_BIAS_MEAN = 7.0
FORGET_NOISE = 0.5
COND_STD = 0.3
SPLIT_SIZES = (BRANCH_W, BRANCH_W,
               BRANCH_W, BRANCH_W, BRANCH_W, N_HEADS, BRANCH_W,
               BRANCH_W, BRANCH_W, BRANCH_W, BRANCH_W,
               N_BRANCH * D_MODEL)
D_IN = sum(SPLIT_SIZES)

kernel_name = 'hybrid_pool_fox_conv_decoder_step'


def rmsnorm(x, g):
    xf = x.astype(jnp.float32)
    y = xf * lax.rsqrt(jnp.mean(xf * xf, axis=-1, keepdims=True) + RMS_EPS)
    return (y * g.astype(jnp.float32)).astype(x.dtype)


def split_cols(z):
    idx, off = [], 0
    for s in SPLIT_SIZES[:-1]:
        off += s
        idx.append(off)
    return jnp.split(z, idx, axis=-1)


def pool_mixer(u, hist, pos0, w_grp, scale):
    B, T, _ = u.shape
    ext = jnp.concatenate([hist, u], axis=1)
    cs = jnp.cumsum(ext.astype(jnp.float32), axis=1)
    cs = jnp.concatenate([jnp.zeros((B, 1, BRANCH_W), jnp.float32), cs], axis=1)
    end = cs[:, POOL_HIST + 1:]
    pos = pos0 + jnp.arange(T)
    means = []
    for g, w in enumerate(POOL_WINDOWS):
        sl = slice(g * POOL_GC, (g + 1) * POOL_GC)
        start = lax.slice_in_dim(cs, POOL_HIST + 1 - w, POOL_HIST + 1 - w + T, axis=1)[..., sl]
        cnt = jnp.minimum(pos + 1, w).astype(jnp.float32)[None, :, None]
        means.append((end[..., sl] - start) / cnt)
    mean = jnp.stack(means, axis=2)
    d = mean - u.reshape(B, T, POOL_GROUPS, POOL_GC).astype(jnp.float32)
    y = jnp.einsum('btgc,gcd->btgd', d.astype(u.dtype), w_grp).reshape(B, T, BRANCH_W)
    return y * scale, ext[:, -POOL_HIST:]


def short_conv(z, hist, w):
    T = z.shape[1]
    ext = jnp.concatenate([hist, z], axis=1)
    y = ext[:, 0:T] * w[0] + ext[:, 1:1 + T] * w[1] + ext[:, 2:2 + T] * w[2]
    return y, ext[:, -CONV_HIST:]


def forgetting_attention(q, k, v, lq, lk, q_pos, k_pos):
    B, T, H, Dh = q.shape
    qb = min(Q_BLOCK, T)
    nb = -(-T // qb)
    pad = nb * qb - T
    if pad:
        q = jnp.pad(q, ((0, 0), (0, pad), (0, 0), (0, 0)))
        lq = jnp.pad(lq, ((0, 0), (0, pad), (0, 0)))
        q_pos = jnp.pad(q_pos, (0, pad), mode='edge')
    qs = q.reshape(B, nb, qb, H, Dh).transpose(1, 0, 2, 3, 4)
    lqs = lq.reshape(B, nb, qb, H).transpose(1, 0, 3, 2)
    ps = q_pos.reshape(nb, qb)
    lkT = lk.transpose(0, 2, 1)[:, :, None, :]
    scale = HEAD_DIM ** -0.5

    def block(args):
        qi, li, pi = args
        s = jnp.einsum('bqhd,bkhd->bhqk', qi, k).astype(jnp.float32) * scale
        s = s + li[..., None] - lkT
        mask = k_pos[None, :] <= pi[:, None]
        s = jnp.where(mask[None, None], s, -jnp.inf)
        p = jax.nn.softmax(s, axis=-1)
        return jnp.einsum('bhqk,bkhd->bqhd', p.astype(v.dtype), v)

    o = lax.map(block, (qs, lqs, ps))
    return o.transpose(1, 0, 2, 3, 4).reshape(B, nb * qb, H, Dh)[:, :T]


def mixer_layer(x, c, hist_pool, hist_conv, kv_past, pos0, norm_g, w_cond, b_cond, w_in, b_f,
                pool_w, pool_scale, conv_w, w_br_a, w_br_b, w_br_c, w_o):
    B, T, _ = x.shape
    mod = (c @ w_cond + b_cond)[:, None, :]
    shift, scale, gate = jnp.split(mod, 3, axis=-1)
    h = rmsnorm(x, norm_g) * (1 + scale) + shift
    z = h @ w_in
    pu, pg, q, k, v, fl, ag, ch, cb, cc, cg, mg = split_cols(z)

    ya, new_pool = pool_mixer(pu, hist_pool, pos0, pool_w, pool_scale)
    ya = (ya * jax.nn.silu(pg)) @ w_br_a

    q = q.reshape(B, T, N_HEADS, HEAD_DIM)
    k = k.reshape(B, T, N_HEADS, HEAD_DIM)
    v = v.reshape(B, T, N_HEADS, HEAD_DIM)
    logf = jax.nn.log_sigmoid((fl + b_f).astype(jnp.float32))
    lnew = jnp.cumsum(logf, axis=1)
    q_pos = pos0 + jnp.arange(T)
    if kv_past is None:
        o = forgetting_attention(q, k, v, lnew, lnew, q_pos, q_pos)
    else:
        kp, vp, lfp = kv_past
        P = kp.shape[1]
        lfp = lfp.astype(jnp.float32)
        a = lfp - lax.cumsum(lfp, axis=1, reverse=True)
        k_all = jnp.concatenate([kp, k], axis=1)
        v_all = jnp.concatenate([vp, v], axis=1)
        lk = jnp.concatenate([a, lnew], axis=1)
        o = forgetting_attention(q, k_all, v_all, lnew, lk, q_pos, jnp.arange(P + T))
    yb = (o.reshape(B, T, BRANCH_W) * jax.nn.silu(ag)) @ w_br_b

    yconv, new_conv = short_conv(cc * ch, hist_conv, conv_w)
    yc = (cb * yconv * jax.nn.silu(cg)) @ w_br_c

    ga, gb, gc = jnp.split(jax.nn.sigmoid(mg), 3, axis=-1)
    merged = ga * ya + gb * yb + gc * yc
    out = x + gate * (merged @ w_o)
    return out, (k, v, logf.astype(x.dtype), new_pool, new_conv)


def setup_inputs(seed: int = 0) -> dict:
    key = jax.random.key(seed)
    ks = jax.random.split(key, 24)
    n_pages = PAST_LEN // PAGE_SIZE
    n_used = DEC_BATCH * n_pages
    n_pool = n_used + n_used // 4

    def nrm(k, shape, s):
        return jax.random.normal(k, shape, jnp.float32) * s

    perm = jax.random.permutation(ks[0], n_pool)
    page_table = perm[:n_used].reshape(DEC_BATCH, n_pages).astype(jnp.int32)
    return {
        'x_prompt': nrm(ks[1], (BATCH, SEQ, D_MODEL), 1.0),
        'x_sample': nrm(ks[2], (DEC_BATCH, DEC_SEQ, D_MODEL), 1.0),
        'cache_k': nrm(ks[3], (DEPTH, n_pool, PAGE_SIZE, N_HEADS, HEAD_DIM), 1.0),
        'cache_v': nrm(ks[4], (DEPTH, n_pool, PAGE_SIZE, N_HEADS, HEAD_DIM), 1.0),
        'cache_logf': jax.nn.log_sigmoid(FORGET_BIAS_MEAN + nrm(ks[5], (DEPTH, n_pool, PAGE_SIZE, N_HEADS), FORGET_NOISE)),
        'state_pool': nrm(ks[6], (DEPTH, DEC_BATCH, POOL_HIST, BRANCH_W), 1.0),
        'state_conv': nrm(ks[7], (DEPTH, DEC_BATCH, CONV_HIST, BRANCH_W), 1.0),
        'page_table': page_table,
        'c_prompt': nrm(ks[8], (BATCH, D_MODEL), 1.0),
        'c_sample': nrm(ks[9], (DEC_BATCH, D_MODEL), 1.0),
        'norm_g': 1.0 + nrm(ks[10], (DEPTH, D_MODEL), 0.1),
        'w_cond': nrm(ks[11], (DEPTH, D_MODEL, 3 * D_MODEL), COND_STD * D_MODEL ** -0.5),
        'b_cond': nrm(ks[12], (DEPTH, 3 * D_MODEL), 0.01),
        'w_in': nrm(ks[13], (DEPTH, D_MODEL, D_IN), D_MODEL ** -0.5),
        'b_f': FORGET_BIAS_MEAN + nrm(ks[14], (DEPTH, N_HEADS), FORGET_NOISE),
        'pool_w': nrm(ks[15], (DEPTH, POOL_GROUPS, POOL_GC, POOL_GC), POOL_GC ** -0.5),
        'pool_scale': 1.0 + nrm(ks[16], (DEPTH, BRANCH_W), 0.1),
        'conv_w': nrm(ks[17], (DEPTH, CONV_W, BRANCH_W), CONV_W ** -0.5),
        'w_br_a': nrm(ks[18], (DEPTH, BRANCH_W, D_MODEL), BRANCH_W ** -0.5),
        'w_br_b': nrm(ks[19], (DEPTH, BRANCH_W, D_MODEL), BRANCH_W ** -0.5),
        'w_br_c': nrm(ks[20], (DEPTH, BRANCH_W, D_MODEL), BRANCH_W ** -0.5),
        'w_o': nrm(ks[21], (DEPTH, D_MODEL, D_MODEL), D_MODEL ** -0.5),
        'final_g': 1.0 + nrm(ks[22], (D_MODEL,), 0.1),
    }


def reference(x_prompt, x_sample, cache_k, cache_v, cache_logf, state_pool, state_conv, page_table,
              c_prompt, c_sample, norm_g, w_cond, b_cond, w_in, b_f, pool_w, pool_scale, conv_w,
              w_br_a, w_br_b, w_br_c, w_o, final_g):
    B = x_prompt.shape[0]
    DB = x_sample.shape[0]
    P = page_table.shape[1] * cache_k.shape[2]
    xp, xs = x_prompt, x_sample
    zero_pool = jnp.zeros((B, POOL_HIST, BRANCH_W), xp.dtype)
    zero_conv = jnp.zeros((B, CONV_HIST, BRANCH_W), xp.dtype)
    st_p, st_s = [], []
    for l in range(DEPTH):
        w = (norm_g[l], w_cond[l], b_cond[l], w_in[l], b_f[l], pool_w[l], pool_scale[l], conv_w[l],
             w_br_a[l], w_br_b[l], w_br_c[l], w_o[l])
        xp, sp = mixer_layer(xp, c_prompt, zero_pool, zero_conv, None, 0, *w)
        kp = cache_k[l][page_table].reshape(DB, P, N_HEADS, HEAD_DIM)
        vp = cache_v[l][page_table].reshape(DB, P, N_HEADS, HEAD_DIM)
        lfp = cache_logf[l][page_table].reshape(DB, P, N_HEADS)
        xs, ss = mixer_layer(xs, c_sample, state_pool[l], state_conv[l], (kp, vp, lfp), P, *w)
        st_p.append(sp)
        st_s.append(ss)
    y_prompt = rmsnorm(xp, final_g)
    y_sample = rmsnorm(xs, final_g)
    new_k_prompt = jnp.stack([s[0] for s in st_p])
    new_v_prompt = jnp.stack([s[1] for s in st_p])
    new_logf_prompt = jnp.stack([s[2] for s in st_p])
    new_pool_prompt = jnp.stack([s[3] for s in st_p])
    new_conv_prompt = jnp.stack([s[4] for s in st_p])
    new_k_sample = jnp.stack([s[0] for s in st_s])
    new_v_sample = jnp.stack([s[1] for s in st_s])
    new_logf_sample = jnp.stack([s[2] for s in st_s])
    new_pool_sample = jnp.stack([s[3] for s in st_s])
    new_conv_sample = jnp.stack([s[4] for s in st_s])
    return (y_prompt, y_sample, new_k_prompt, new_v_prompt, new_logf_prompt, new_pool_prompt, new_conv_prompt,
            new_k_sample, new_v_sample, new_logf_sample, new_pool_sample, new_conv_sample)
```

```python
import functools

import jax
import jax.numpy as jnp
from jax import lax
from jax.experimental import pallas as pl
from jax.experimental.pallas import tpu as pltpu

D_MODEL = 1024
BRANCH_W = D_MODEL // 2
POOL_WINDOWS = (2, 4, 8, 16)
POOL_GC = BRANCH_W // len(POOL_WINDOWS)
POOL_HIST = max(POOL_WINDOWS) - 1
N_HEADS = 8
HEAD_DIM = BRANCH_W // N_HEADS
CONV_W = 3
CONV_HIST = CONV_W - 1
RMS_EPS = 1e-6

LANES = 128
SUBLANES = 8
POOL_PAD = 16
CONV_PAD = 8
VMEM_LIMIT = 56 * 1024 * 1024

_OFF_PU, _OFF_PG, _OFF_Q, _OFF_K, _OFF_V = 0, 512, 1024, 1536, 2048
_OFF_AG, _OFF_CH, _OFF_CB, _OFF_CC, _OFF_CG, _OFF_MG = 2560, 3072, 3584, 4096, 4608, 5120
_W_MAIN = _OFF_MG + 3 * D_MODEL

NEG_BIG = -1e30
F32 = jnp.float32
BF16 = jnp.bfloat16


def _const_spec(shape, index_map):
    return pl.BlockSpec(shape, index_map, pipeline_mode=pl.Buffered(1))


def _silu(x):
    return x * jax.nn.sigmoid(x)


def _log_sigmoid(x):
    return jnp.minimum(x, 0.0) - jnp.log1p(jnp.exp(-jnp.abs(x)))


def _mod_body(c_ref, w_ref, b_ref, o_ref):
    o_ref[...] = jnp.dot(c_ref[...].astype(BF16), w_ref[...], preferred_element_type=F32) + b_ref[...]


def _modulation(c_all, w_cond_b, b_cond):
    depth = w_cond_b.shape[0]
    n = c_all.shape[0]
    return pl.pallas_call(
        _mod_body,
        grid=(depth,),
        in_specs=[
            pl.BlockSpec((n, D_MODEL), lambda l: (0, 0)),
            pl.BlockSpec((None, D_MODEL, 3 * D_MODEL), lambda l: (l, 0, 0)),
            pl.BlockSpec((None, 1, 3 * D_MODEL), lambda l: (l, 0, 0)),
        ],
        out_specs=pl.BlockSpec((None, n, 3 * D_MODEL), lambda l: (l, 0, 0)),
        out_shape=jax.ShapeDtypeStruct((depth, n, 3 * D_MODEL), F32),
        compiler_params=pltpu.CompilerParams(dimension_semantics=("arbitrary",),
                                             vmem_limit_bytes=VMEM_LIMIT),
        name="modulation",
    )(c_all, w_cond_b, b_cond.reshape(depth, 1, 3 * D_MODEL))


def _in_proj_body(x_ref, mod_ref, g_ref, wm_ref, wf_ref, bf_ref, poolw_ref, pscale_ref, convw_ref,
                  wa_ref, wc_ref, hp_ref, hc_ref,
                  q_ref, k_ref, v_ref, ag_ref, part_ref, gb_ref, logf_ref, npool_ref, nconv_ref,
                  pext_ref, cext_ref, *, nseg, seg_len, pos0):
    t = pl.program_id(1)
    rows = nseg * seg_len

    @pl.when(t == 0)
    def _():
        pext_ref[:, 0:1, :] = jnp.zeros((nseg, 1, BRANCH_W), F32)
        pext_ref[:, 1:POOL_PAD, :] = hp_ref[...]
        cext_ref[:, 0:CONV_PAD - CONV_HIST, :] = jnp.zeros((nseg, CONV_PAD - CONV_HIST, BRANCH_W), F32)
        cext_ref[:, CONV_PAD - CONV_HIST:CONV_PAD, :] = hc_ref[...]

    xf = x_ref[...].reshape(rows, D_MODEL)
    ms = jnp.mean(xf * xf, axis=-1, keepdims=True)
    y = (xf * lax.rsqrt(ms + RMS_EPS)) * g_ref[...]
    mod = mod_ref[...]
    shift = mod[:, :, 0:D_MODEL]
    scale = mod[:, :, D_MODEL:2 * D_MODEL]
    h = y.reshape(nseg, seg_len, D_MODEL) * (1.0 + scale) + shift
    hb = h.reshape(rows, D_MODEL).astype(BF16)

    def proj(off, width):
        return jnp.dot(hb, wm_ref[:, off:off + width], preferred_element_type=F32)

    zp = proj(_OFF_PU, 2 * BRANCH_W)
    pu = zp[:, :BRANCH_W]
    pg = zp[:, BRANCH_W:]
    pext_ref[:, POOL_PAD:POOL_PAD + seg_len, :] = pu.reshape(nseg, seg_len, BRANCH_W)
    pos = pos0 + t * seg_len + lax.broadcasted_iota(jnp.int32, (1, seg_len, POOL_GC), 1)
    ys = []
    for g, w in enumerate(POOL_WINDOWS):
        lanes = slice(g * POOL_GC, (g + 1) * POOL_GC)
        cur = pext_ref[:, POOL_PAD:POOL_PAD + seg_len, lanes]
        acc = cur
        for i in range(1, w):
            acc = acc + pext_ref[:, POOL_PAD - i:POOL_PAD - i + seg_len, lanes]
        cnt = jnp.minimum(pos + 1, w).astype(F32)
        d = acc / cnt - cur
        ys.append(jnp.dot(d.reshape(rows, POOL_GC).astype(BF16), poolw_ref[g],
                          preferred_element_type=F32))
    ya_in = (jnp.concatenate(ys, axis=-1) * pscale_ref[...]) * _silu(pg)
    ya = jnp.dot(ya_in.astype(BF16), wa_ref[...], preferred_element_type=F32)
    npool_ref[...] = pext_ref[:, seg_len + POOL_PAD - POOL_HIST:seg_len + POOL_PAD, :]
    pool_tail = pext_ref[:, seg_len:seg_len + POOL_PAD, :]
    pext_ref[:, 0:POOL_PAD, :] = pool_tail

    zc = proj(_OFF_CH, 4 * BRANCH_W)
    ch = zc[:, 0:BRANCH_W]
    cb = zc[:, BRANCH_W:2 * BRANCH_W]
    cc = zc[:, 2 * BRANCH_W:3 * BRANCH_W]
    cg = zc[:, 3 * BRANCH_W:]
    cext_ref[:, CONV_PAD:CONV_PAD + seg_len, :] = (cc * ch).reshape(nseg, seg_len, BRANCH_W)
    cw = convw_ref[...]
    yconv = None
    for kk in range(CONV_W):
        o = CONV_PAD - CONV_HIST + kk
        term = cext_ref[:, o:o + seg_len, :] * cw[kk:kk + 1, :]
        yconv = term if yconv is None else yconv + term
    yc_in = (cb * yconv.reshape(rows, BRANCH_W)) * _silu(cg)
    yc = jnp.dot(yc_in.astype(BF16), wc_ref[...], preferred_element_type=F32)
    nconv_ref[...] = cext_ref[:, seg_len + CONV_PAD - CONV_HIST:seg_len + CONV_PAD, :]
    conv_tail = cext_ref[:, seg_len:seg_len + CONV_PAD, :]
    cext_ref[:, 0:CONV_PAD, :] = conv_tail

    gates = jax.nn.sigmoid(proj(_OFF_MG, 3 * D_MODEL))
    ga = gates[:, 0:D_MODEL]
    gb = gates[:, D_MODEL:2 * D_MODEL]
    gc = gates[:, 2 * D_MODEL:]
    part_ref[...] = (ga * ya + gc * yc).reshape(nseg, seg_len, D_MODEL)
    gb_ref[...] = gb.reshape(nseg, seg_len, D_MODEL)

    zq = proj(_OFF_Q, 3 * BRANCH_W)
    q_ref[...] = (zq[:, 0:BRANCH_W] * (HEAD_DIM ** -0.5)).reshape(nseg, seg_len, BRANCH_W).astype(q_ref.dtype)
    k_ref[...] = zq[:, BRANCH_W:2 * BRANCH_W].reshape(nseg, seg_len, BRANCH_W)
    v_ref[...] = zq[:, 2 * BRANCH_W:].reshape(nseg, seg_len, BRANCH_W)
    ag_ref[...] = proj(_OFF_AG, BRANCH_W).reshape(nseg, seg_len, BRANCH_W)
    zf = jnp.dot(hb, wf_ref[...], preferred_element_type=F32)
    logf_ref[...] = _log_sigmoid(zf + bf_ref[...]).reshape(nseg, seg_len, LANES)


def _in_proj(x, mod, hist_pool, hist_conv, lw, l, *, nseg, seg_len, pos0, q_dtype):
    nb, ltot, _ = x.shape
    grid = (nb // nseg, ltot // seg_len)
    tile = lambda c: pl.BlockSpec((nseg, seg_len, c), lambda b, t: (b, t, 0))
    per_b = lambda r, c: pl.BlockSpec((nseg, r, c), lambda b, t: (b, 0, 0))
    wl = lambda shape: _const_spec((None,) + shape, lambda b, t: (l,) + (0,) * len(shape))
    out_c = (BRANCH_W, BRANCH_W, BRANCH_W, BRANCH_W, D_MODEL, D_MODEL, LANES)
    out_dt = (q_dtype, F32, F32, F32, F32, F32, F32)
    out_shape = [jax.ShapeDtypeStruct((nb, ltot, c), dt) for c, dt in zip(out_c, out_dt)]
    out_shape += [jax.ShapeDtypeStruct((nb, POOL_HIST, BRANCH_W), F32),
                  jax.ShapeDtypeStruct((nb, CONV_HIST, BRANCH_W), F32)]
    out_specs = [tile(c) for c in out_c] + [per_b(POOL_HIST, BRANCH_W), per_b(CONV_HIST, BRANCH_W)]
    return pl.pallas_call(
        functools.partial(_in_proj_body, nseg=nseg, seg_len=seg_len, pos0=pos0),
        grid=grid,
        in_specs=[
            tile(D_MODEL),
            per_b(1, 3 * D_MODEL),
            wl((1, D_MODEL)),
            wl((D_MODEL, _W_MAIN)),
            wl((D_MODEL, LANES)),
            wl((1, LANES)),
            wl((len(POOL_WINDOWS), POOL_GC, POOL_GC)),
            wl((1, BRANCH_W)),
            wl((CONV_W, BRANCH_W)),
            wl((BRANCH_W, D_MODEL)),
            wl((BRANCH_W, D_MODEL)),
            per_b(POOL_HIST, BRANCH_W),
            per_b(CONV_HIST, BRANCH_W),
        ],
        out_specs=out_specs,
        out_shape=out_shape,
        scratch_shapes=[pltpu.VMEM((nseg, POOL_PAD + seg_len, BRANCH_W), F32),
                        pltpu.VMEM((nseg, CONV_PAD + seg_len, BRANCH_W), F32)],
        compiler_params=pltpu.CompilerParams(dimension_semantics=("arbitrary", "arbitrary"),
                                             vmem_limit_bytes=VMEM_LIMIT),
        name="in_proj",
    )(x, mod, lw["norm_g"], lw["w_main"], lw["w_f"], lw["b_f"], lw["pool_w"], lw["pool_scale"],
      lw["conv_w"], lw["w_br_a"], lw["w_br_c"], hist_pool, hist_conv)


def _lane_cumsum(x):
    n = x.shape[-1] // LANES
    lane = lax.broadcasted_iota(jnp.int32, (x.shape[0], LANES), 1)
    out = []
    carry = None
    for c in range(n):
        blk = x[:, c * LANES:(c + 1) * LANES]
        sh = 1
        while sh < LANES:
            blk = blk + jnp.where(lane >= sh, pltpu.roll(blk, sh, 1), 0.0)
            sh *= 2
        if carry is not None:
            blk = blk + carry
        carry = blk[:, LANES - 1:LANES]
        out.append(blk)
    return jnp.concatenate(out, axis=-1)


def _prompt_attn_body(q_ref, k_ref, v_ref, logf_ref, o_ref, kb_ref, vb_ref, lk_ref, *, tq):
    i = pl.program_id(1)

    @pl.when(i == 0)
    def _():
        kb_ref[...] = k_ref[...].astype(BF16)
        vb_ref[...] = v_ref[...].astype(BF16)
        lft = logf_ref[...].T
        lk_ref[...] = _lane_cumsum(lft[0:N_HEADS, :])

    lane = lax.broadcasted_iota(jnp.int32, (tq, LANES), 1)
    low = lane < HEAD_DIM
    row_id = lax.broadcasted_iota(jnp.int32, (tq, tq), 0)
    col_id = lax.broadcasted_iota(jnp.int32, (tq, tq), 1)
    causal = col_id <= row_id

    for pair in range(N_HEADS // 2):
        slab = slice(pair * LANES, (pair + 1) * LANES)
        q2 = q_ref[:, slab]
        for half in range(2):
            head = 2 * pair + half
            sel = low if half == 0 else jnp.logical_not(low)
            qh = jnp.where(sel, q2, jnp.zeros_like(q2))

            def scores(kt):
                start = pl.multiple_of(kt * tq, tq)
                k2 = kb_ref[pl.ds(start, tq), slab]
                s = lax.dot_general(qh, k2, (((1,), (1,)), ((), ())), preferred_element_type=F32)
                return s - lk_ref[head:head + 1, pl.ds(start, tq)], start

            def update(carry, s, start):
                m, l, acc = carry
                m_new = jnp.maximum(m, jnp.max(s, axis=-1, keepdims=True))
                alpha = jnp.exp(m - m_new)
                p = jnp.exp(s - m_new)
                l = alpha * l + jnp.sum(p, axis=-1, keepdims=True)
                v2 = vb_ref[pl.ds(start, tq), slab]
                acc = alpha * acc + jnp.dot(p.astype(BF16), v2, preferred_element_type=F32)
                return m_new, l, acc

            def step(kt, carry):
                s, start = scores(kt)
                return update(carry, s, start)

            init = (jnp.full((tq, 1), NEG_BIG, F32), jnp.zeros((tq, 1), F32), jnp.zeros((tq, LANES), F32))
            carry = lax.fori_loop(0, i, step, init)
            s, start = scores(i)
            m, l, acc = update(carry, jnp.where(causal, s, NEG_BIG), start)
            oh = acc / l
            if half == 0:
                o_ref[:, slab] = oh
            else:
                o_ref[:, slab] = jnp.where(low, o_ref[:, slab], oh)


def _prompt_attn(q, k, v, logf, *, tq):
    b, t, _ = q.shape
    return pl.pallas_call(
        functools.partial(_prompt_attn_body, tq=tq),
        grid=(b, t // tq),
        in_specs=[
            pl.BlockSpec((None, tq, BRANCH_W), lambda bi, i: (bi, i, 0)),
            pl.BlockSpec((None, t, BRANCH_W), lambda bi, i: (bi, 0, 0)),
            pl.BlockSpec((None, t, BRANCH_W), lambda bi, i: (bi, 0, 0)),
            pl.BlockSpec((None, t, LANES), lambda bi, i: (bi, 0, 0)),
        ],
        out_specs=pl.BlockSpec((None, tq, BRANCH_W), lambda bi, i: (bi, i, 0)),
        out_shape=jax.ShapeDtypeStruct((b, t, BRANCH_W), F32),
        scratch_shapes=[pltpu.VMEM((t, BRANCH_W), BF16), pltpu.VMEM((t, BRANCH_W), BF16),
                        pltpu.VMEM((N_HEADS, t), F32)],
        compiler_params=pltpu.CompilerParams(dimension_semantics=("arbitrary", "arbitrary"),
                                             vmem_limit_bytes=VMEM_LIMIT),
        name="prompt_attn",
    )(q, k, v, logf)


def _sample_attn_body(pt_ref, q_ref, kn_ref, vn_ref, lfn_ref, *rest, pps, page, nchunk):
    k_refs = rest[0:pps]
    v_refs = rest[pps:2 * pps]
    lf_refs = rest[2 * pps:3 * pps]
    o_ref = rest[3 * pps]
    qbd_ref, m_ref, l_ref, acc_ref, carry_ref = rest[3 * pps + 1:]
    del pt_ref
    j = pl.program_id(1)
    dec = q_ref.shape[0]
    nrow = dec * N_HEADS
    head_of_lane = lax.broadcasted_iota(jnp.int32, (N_HEADS, BRANCH_W), 1) // HEAD_DIM
    head_mask = head_of_lane == lax.broadcasted_iota(jnp.int32, (N_HEADS, BRANCH_W), 0)

    @pl.when(j == 0)
    def _():
        q = q_ref[...]
        blocks = [jnp.where(head_mask, jnp.broadcast_to(q[tt:tt + 1, :], (N_HEADS, BRANCH_W)), 0.0)
                  for tt in range(dec)]
        qbd_ref[...] = jnp.concatenate(blocks, axis=0).astype(BF16)
        m_ref[...] = jnp.full((nrow, 1), NEG_BIG, F32)
        l_ref[...] = jnp.zeros((nrow, 1), F32)
        acc_ref[...] = jnp.zeros((nrow, BRANCH_W), F32)
        carry_ref[...] = jnp.zeros((N_HEADS, LANES), F32)

    qbd = qbd_ref[...]

    def attend(kb, vb, bias):
        s = lax.dot_general(qbd, kb, (((1,), (1,)), ((), ())), preferred_element_type=F32) + bias
        m = m_ref[...]
        m_new = jnp.maximum(m, jnp.max(s, axis=-1, keepdims=True))
        alpha = jnp.exp(m - m_new)
        p = jnp.exp(s - m_new)
        l_ref[...] = alpha * l_ref[...] + jnp.sum(p, axis=-1, keepdims=True)
        acc_ref[...] = alpha * acc_ref[...] + jnp.dot(p.astype(BF16), vb, preferred_element_type=F32)
        m_ref[...] = m_new

    lane = lax.broadcasted_iota(jnp.int32, (N_HEADS, page), 1)
    for idx in range(pps - 1, -1, -1):
        lft = lf_refs[idx][...]
        later = jnp.where(lane < page - 1, pltpu.roll(lft, page - 1, 1), 0.0)
        sh = 1
        while sh < page:
            later = later + jnp.where(lane < page - sh, pltpu.roll(later, page - sh, 1), 0.0)
            sh *= 2
        suffix = later + carry_ref[...]
        carry_ref[...] = jnp.broadcast_to(suffix[:, 0:1] + lft[:, 0:1], (N_HEADS, LANES))
        bias = jnp.concatenate([suffix] * dec, axis=0)
        attend(k_refs[idx][...].astype(BF16), v_refs[idx][...].astype(BF16), bias)

    @pl.when(j == nchunk - 1)
    def _():
        lfn = lfn_ref[...]
        run = lfn[0:1, :]
        cum = [run]
        for tt in range(1, dec):
            run = run + lfn[tt:tt + 1, :]
            cum.append(run)
        cum.append(jnp.zeros((LANES - dec, LANES), F32))
        lnew_t = jnp.concatenate(cum, axis=0).T[0:N_HEADS, :]
        key_id = lax.broadcasted_iota(jnp.int32, (nrow, LANES), 1)
        t_id = lax.broadcasted_iota(jnp.int32, (nrow, LANES), 0) // N_HEADS
        bias = jnp.where(key_id <= t_id, -jnp.concatenate([lnew_t] * dec, axis=0), NEG_BIG)
        zpad = jnp.zeros((LANES - dec, BRANCH_W), F32)
        kpad = jnp.concatenate([kn_ref[...], zpad], axis=0).astype(BF16)
        vpad = jnp.concatenate([vn_ref[...], zpad], axis=0).astype(BF16)
        attend(kpad, vpad, bias)
        o = acc_ref[...] / l_ref[...]
        o = jnp.where(jnp.concatenate([head_mask] * dec, axis=0), o, 0.0)
        o_ref[...] = jnp.sum(o.reshape(dec, N_HEADS, BRANCH_W), axis=1)


def _sample_attn(page_table, q, k_new, v_new, logf_new, cache_k, cache_v, cache_lft, l, *, pps):
    db, dec, _ = q.shape
    n_pages = page_table.shape[1]
    page = cache_k.shape[2]
    nchunk = n_pages // pps

    def paged(idx, rows, cols):
        def imap(b, j, pt):
            return (l, pt[b, (nchunk - 1 - j) * pps + idx], 0, 0)
        return pl.BlockSpec((None, None, rows, cols), imap)

    per_b = lambda c: pl.BlockSpec((None, dec, c), lambda b, j, pt: (b, 0, 0))
    in_specs = [per_b(BRANCH_W), per_b(BRANCH_W), per_b(BRANCH_W), per_b(LANES)]
    in_specs += [paged(i, page, BRANCH_W) for i in range(pps)]
    in_specs += [paged(i, page, BRANCH_W) for i in range(pps)]
    in_specs += [paged(i, N_HEADS, page) for i in range(pps)]
    nrow = dec * N_HEADS
    grid_spec = pltpu.PrefetchScalarGridSpec(
        num_scalar_prefetch=1,
        grid=(db, nchunk),
        in_specs=in_specs,
        out_specs=per_b(BRANCH_W),
        scratch_shapes=[pltpu.VMEM((nrow, BRANCH_W), BF16), pltpu.VMEM((nrow, 1), F32),
                        pltpu.VMEM((nrow, 1), F32), pltpu.VMEM((nrow, BRANCH_W), F32),
                        pltpu.VMEM((N_HEADS, LANES), F32)],
    )
    return pl.pallas_call(
        functools.partial(_sample_attn_body, pps=pps, page=page, nchunk=nchunk),
        grid_spec=grid_spec,
        out_shape=jax.ShapeDtypeStruct((db, dec, BRANCH_W), F32),
        compiler_params=pltpu.CompilerParams(dimension_semantics=("arbitrary", "arbitrary"),
                                             vmem_limit_bytes=VMEM_LIMIT),
        name="sample_attn",
    )(page_table, q, k_new, v_new, logf_new, *([cache_k] * pps), *([cache_v] * pps), *([cache_lft] * pps))


def _out_proj_body(o_ref, ag_ref, part_ref, gb_ref, x_ref, mod_ref, wb_ref, wo_ref, fg_ref, y_ref,
                   *, nseg, seg_len, final_norm):
    rows = nseg * seg_len
    yb_in = o_ref[...].reshape(rows, BRANCH_W) * _silu(ag_ref[...].reshape(rows, BRANCH_W))
    yb = jnp.dot(yb_in.astype(BF16), wb_ref[...], preferred_element_type=F32)
    merged = part_ref[...].reshape(rows, D_MODEL) + gb_ref[...].reshape(rows, D_MODEL) * yb
    z = jnp.dot(merged.astype(BF16), wo_ref[...], preferred_element_type=F32)
    gate = mod_ref[...][:, :, 2 * D_MODEL:]
    out = x_ref[...] + gate * z.reshape(nseg, seg_len, D_MODEL)
    if final_norm:
        ms = jnp.mean(out * out, axis=-1, keepdims=True)
        out = (out * lax.rsqrt(ms + RMS_EPS)) * fg_ref[...]
    y_ref[...] = out


def _out_proj(o, ag, part, gb, x, mod, lw, l, final_g, *, nseg, seg_len, final_norm):
    nb, ltot, _ = x.shape
    tile = lambda c: pl.BlockSpec((nseg, seg_len, c), lambda b, t: (b, t, 0))
    wl = lambda shape: _const_spec((None,) + shape, lambda b, t: (l,) + (0,) * len(shape))
    return pl.pallas_call(
        functools.partial(_out_proj_body, nseg=nseg, seg_len=seg_len, final_norm=final_norm),
        grid=(nb // nseg, ltot // seg_len),
        in_specs=[tile(BRANCH_W), tile(BRANCH_W), tile(D_MODEL), tile(D_MODEL), tile(D_MODEL),
                  pl.BlockSpec((nseg, 1, 3 * D_MODEL), lambda b, t: (b, 0, 0)),
                  wl((BRANCH_W, D_MODEL)), wl((D_MODEL, D_MODEL)),
                  _const_spec((1, D_MODEL), lambda b, t: (0, 0))],
        out_specs=tile(D_MODEL),
        out_shape=jax.ShapeDtypeStruct((nb, ltot, D_MODEL), F32),
        compiler_params=pltpu.CompilerParams(dimension_semantics=("arbitrary", "arbitrary"),
                                             vmem_limit_bytes=VMEM_LIMIT),
        name="out_proj",
    )(o, ag, part, gb, x, mod, lw["w_br_b"], lw["w_o"], final_g)


PROMPT_TILE = 256
PAGES_PER_STEP = 8


def kernel(x_prompt, x_sample, cache_k, cache_v, cache_logf, state_pool, state_conv, page_table,
           c_prompt, c_sample, norm_g, w_cond, b_cond, w_in, b_f, pool_w, pool_scale, conv_w,
           w_br_a, w_br_b, w_br_c, w_o, final_g):
    depth = w_in.shape[0]
    b, seq, _ = x_prompt.shape
    db, dec, _ = x_sample.shape
    n_pool, page = cache_k.shape[1], cache_k.shape[2]
    past = page_table.shape[1] * page

    fl0 = 5 * BRANCH_W
    w_main = jnp.concatenate([w_in[:, :, :fl0], w_in[:, :, fl0 + N_HEADS:]], axis=-1).astype(BF16)
    pad = LANES - N_HEADS
    lw = {
        "norm_g": norm_g.reshape(depth, 1, D_MODEL),
        "w_main": w_main,
        "w_f": jnp.pad(w_in[:, :, fl0:fl0 + N_HEADS], ((0, 0), (0, 0), (0, pad))).astype(BF16),
        "b_f": jnp.pad(b_f, ((0, 0), (0, pad))).reshape(depth, 1, LANES),
        "pool_w": pool_w.astype(BF16),
        "pool_scale": pool_scale.reshape(depth, 1, BRANCH_W),
        "conv_w": conv_w,
        "w_br_a": w_br_a.astype(BF16),
        "w_br_b": w_br_b.astype(BF16),
        "w_br_c": w_br_c.astype(BF16),
        "w_o": w_o.astype(BF16),
    }
    final_g2 = final_g.reshape(1, D_MODEL)
    mod = _modulation(jnp.concatenate([c_prompt, c_sample], axis=0), w_cond.astype(BF16), b_cond)
    mod_p = mod[:, :b].reshape(depth, b, 1, 3 * D_MODEL)
    mod_s = mod[:, b:].reshape(depth, db, 1, 3 * D_MODEL)

    ck = cache_k.reshape(depth, n_pool, page, BRANCH_W)
    cv = cache_v.reshape(depth, n_pool, page, BRANCH_W)
    clft = jnp.swapaxes(cache_logf, 2, 3)

    zero_pool = jnp.zeros((b, POOL_HIST, BRANCH_W), F32)
    zero_conv = jnp.zeros((b, CONV_HIST, BRANCH_W), F32)
    xp, xs = x_prompt, x_sample
    outs_p, outs_s = [], []
    for l in range(depth):
        last = l == depth - 1
        q, k, v, ag, part, gb, logf, npool, nconv = _in_proj(
            xp, mod_p[l], zero_pool, zero_conv, lw, l, nseg=1, seg_len=PROMPT_TILE, pos0=0, q_dtype=BF16)
        o = _prompt_attn(q, k, v, logf, tq=PROMPT_TILE)
        xp = _out_proj(o, ag, part, gb, xp, mod_p[l], lw, l, final_g2,
                       nseg=1, seg_len=PROMPT_TILE, final_norm=last)
        outs_p.append((k, v, logf[..., :N_HEADS], npool, nconv))

        q, k, v, ag, part, gb, logf, npool, nconv = _in_proj(
            xs, mod_s[l], state_pool[l], state_conv[l], lw, l, nseg=db, seg_len=dec, pos0=past, q_dtype=F32)
        o = _sample_attn(page_table, q, k, v, logf, ck, cv, clft, l, pps=PAGES_PER_STEP)
        xs = _out_proj(o, ag, part, gb, xs, mod_s[l], lw, l, final_g2,
                       nseg=db, seg_len=dec, final_norm=last)
        outs_s.append((k, v, logf[..., :N_HEADS], npool, nconv))

    def stacked(outs, idx, shape):
        return jnp.stack([o[idx] for o in outs]).reshape(shape)

    return (xp, xs,
            stacked(outs_p, 0, (depth, b, seq, N_HEADS, HEAD_DIM)),
            stacked(outs_p, 1, (depth, b, seq, N_HEADS, HEAD_DIM)),
            stacked(outs_p, 2, (depth, b, seq, N_HEADS)),
            stacked(outs_p, 3, (depth, b, POOL_HIST, BRANCH_W)),
            stacked(outs_p, 4, (depth, b, CONV_HIST, BRANCH_W)),
            stacked(outs_s, 0, (depth, db, dec, N_HEADS, HEAD_DIM)),
            stacked(outs_s, 1, (depth, db, dec, N_HEADS, HEAD_DIM)),
            stacked(outs_s, 2, (depth, db, dec, N_HEADS)),
            stacked(outs_s, 3, (depth, db, POOL_HIST, BRANCH_W)),
            stacked(outs_s, 4, (depth, db, CONV_HIST, BRANCH_W)))
```
